```python
import jax, jax.numpy as jnp
from jax import lax
import numpy as np

D_MODEL = 4096
BATCH = 4
SEQ = 2048
DEPTH = 2
DEC_BATCH = 128
DEC_SEQ = 4
PAST_LEN = 16384
PAGE_SIZE = 128

HEAD_DIM = 128
D_MIX = D_MODEL
N_HEADS = D_MIX // HEAD_DIM
N_HEADS_A = (N_HEADS * 3) // 8
N_HEADS_B = (N_HEADS * 5) // 16
N_HEADS_C = N_HEADS - N_HEADS_A - N_HEADS_B
D_A = N_HEADS_A * HEAD_DIM
D_B = N_HEADS_B * HEAD_DIM
D_C = N_HEADS_C * HEAD_DIM
CONV_A = 31
CONV_C = 3
CHUNK = 128
D_IN = 2 * D_A + 2 * D_B + 3 * D_C
SPLITS = [D_A, 2 * D_A, 2 * D_A + D_B, 2 * D_A + 2 * D_B,
          2 * D_A + 2 * D_B + D_C, 2 * D_A + 2 * D_B + 2 * D_C]
D_FF = 11008
N_EXPERTS = 8
TOP_K = 2
D_FF_EXPERT = 14336
MOE_BLOCK = 128
RMS_EPS = 1e-6
LN_EPS = 1e-5

kernel_name = "hybrid_conv_gmlp_shortconv_adaln_moe_step"


def _rmsnorm(x, g):
    x32 = x.astype(jnp.float32)
    y = x32 * lax.rsqrt(jnp.mean(x32 * x32, axis=-1, keepdims=True) + RMS_EPS)
    return (y * g.astype(jnp.float32)).astype(x.dtype)


def _layernorm(x, g, b):
    x32 = x.astype(jnp.float32)
    mu = jnp.mean(x32, axis=-1, keepdims=True)
    xc = x32 - mu
    var = jnp.mean(xc * xc, axis=-1, keepdims=True)
    y = xc * lax.rsqrt(var + LN_EPS) * g.astype(jnp.float32) + b.astype(jnp.float32)
    return y.astype(x.dtype)


def _causal_dwconv(x_ext, w):
    ch = w.shape[1]
    return lax.conv_general_dilated(
        x_ext, w[:, None, :].astype(x_ext.dtype), window_strides=(1,), padding='VALID',
        dimension_numbers=('NWC', 'WIO', 'NWC'), feature_group_count=ch)


def _chunk_mix(v, w_s, b_s):
    n, L, dv = v.shape
    lc = min(L, CHUNK)
    vh = v.reshape(n, L // lc, lc, N_HEADS_B, HEAD_DIM)
    w = jnp.tril(w_s[:, :lc, :lc]).astype(v.dtype)
    out = jnp.einsum('hpq,bnqhd->bnphd', w, vh)
    out = out + b_s[:, :lc].T.astype(v.dtype)[None, None, :, :, None]
    return out.reshape(n, L, dv)


def _mixers(h, buf_a, buf_c, w_in, w_out, conv_a_w, conv_a_b, ln_a_g, ln_a_b,
            ln_v_g, ln_v_b, w_spatial, b_spatial, conv_c_w):
    proj = h @ w_in
    a_val, a_gate, u, v, bg, cg, hc = jnp.split(proj, SPLITS, axis=-1)
    a_glu = a_val * jax.nn.sigmoid(a_gate)
    a_ext = jnp.concatenate([buf_a.astype(a_glu.dtype), a_glu], axis=1)
    a = _causal_dwconv(a_ext, conv_a_w) + conv_a_b.astype(a_glu.dtype)
    a = jax.nn.silu(_layernorm(a, ln_a_g, ln_a_b))
    v = _layernorm(v, ln_v_g, ln_v_b)
    b = u * _chunk_mix(v, w_spatial, b_spatial)
    s = cg * hc
    s_ext = jnp.concatenate([buf_c.astype(s.dtype), s], axis=1)
    cc = bg * _causal_dwconv(s_ext, conv_c_w)
    out = jnp.concatenate([a, b, cc], axis=-1) @ w_out
    return out, a_ext[:, -(CONV_A - 1):], s_ext[:, -(CONV_C - 1):], v


def _swiglu(h, w_gate, w_up, w_down):
    return (jax.nn.silu(h @ w_gate) * (h @ w_up)) @ w_down


def _moe(h, w_router, w_gate, w_up, w_down):
    n, L, d = h.shape
    xt = h.reshape(n * L, d)
    T = xt.shape[0]
    logits = (xt @ w_router).astype(jnp.float32)
    top_vals, top_idx = lax.top_k(logits, TOP_K)
    gates = jax.nn.softmax(top_vals, axis=-1)
    A = T * TOP_K
    exp_flat = top_idx.reshape(A)
    tok_flat = (jnp.arange(A, dtype=jnp.int32) // TOP_K)
    gate_flat = gates.reshape(A)
    order = jnp.argsort(exp_flat, stable=True)
    exp_sorted = exp_flat[order]
    counts = jnp.bincount(exp_flat, length=N_EXPERTS)
    starts = jnp.cumsum(counts) - counts
    padded = ((counts + MOE_BLOCK - 1) // MOE_BLOCK) * MOE_BLOCK
    pad_ends = jnp.cumsum(padded)
    pad_starts = pad_ends - padded
    rank = jnp.arange(A) - starts[exp_sorted]
    dest = pad_starts[exp_sorted] + rank
    n_blocks = (A + N_EXPERTS * (MOE_BLOCK - 1) + MOE_BLOCK - 1) // MOE_BLOCK
    cap = n_blocks * MOE_BLOCK
    slot_tok = jnp.zeros((cap,), jnp.int32).at[dest].set(tok_flat[order])
    slot_gate = jnp.zeros((cap,), jnp.float32).at[dest].set(gate_flat[order])
    block_exp = jnp.minimum(
        jnp.searchsorted(pad_ends, jnp.arange(n_blocks) * MOE_BLOCK, side='right'),
        N_EXPERTS - 1)

    def block_fn(args):
        tok, e = args
        xb = xt[tok]
        hb = jax.nn.silu(xb @ w_gate[e]) * (xb @ w_up[e])
        return hb @ w_down[e]

    ys = lax.map(block_fn, (slot_tok.reshape(n_blocks, MOE_BLOCK), block_exp))
    ys = ys.reshape(cap, d) * slot_gate[:, None].astype(ys.dtype)
    out = jnp.zeros_like(xt).at[slot_tok].add(ys)
    return out.reshape(n, L, d)


def _trunk(x, c, buf_a, buf_c, w_mod, b_mod, g_norm1, g_norm2, w_in, w_out,
           conv_a_w, conv_a_b, ln_a_g, ln_a_b, ln_v_g, ln_v_b, w_spatial, b_spatial,
           conv_c_w, w_gate_dense, w_up_dense, w_down_dense, w_router, w_gate_exp,
           w_up_exp, w_down_exp, g_final):
    new_a, new_c, new_v = [], [], []
    for l in range(DEPTH):
        mod = jax.nn.silu(c) @ w_mod[l] + b_mod[l]
        sh1, sc1, g1, sh2, sc2, g2 = [m[:, None, :] for m in jnp.split(mod, 6, axis=-1)]
        h = _rmsnorm(x, g_norm1[l]) * (1 + sc1) + sh1
        mix, ba, bc, v = _mixers(h, buf_a[l], buf_c[l], w_in[l], w_out[l], conv_a_w[l],
                                 conv_a_b[l], ln_a_g[l], ln_a_b[l], ln_v_g[l], ln_v_b[l],
                                 w_spatial[l], b_spatial[l], conv_c_w[l])
        x = x + g1 * mix
        h = _rmsnorm(x, g_norm2[l]) * (1 + sc2) + sh2
        if l % 2 == 0:
            f = _swiglu(h, w_gate_dense[l // 2], w_up_dense[l // 2], w_down_dense[l // 2])
        else:
            f = _moe(h, w_router[l // 2], w_gate_exp[l // 2], w_up_exp[l // 2],
                     w_down_exp[l // 2])
        x = x + g2 * f
        new_a.append(ba)
        new_c.append(bc)
        new_v.append(v)
    y = _rmsnorm(x, g_final)
    return y, jnp.stack(new_a), jnp.stack(new_c), jnp.stack(new_v)


def setup_inputs(seed: int = 0) -> dict:
    key = jax.random.key(seed)
    keys = jax.random.split(key, 32)
    f32 = jnp.float32

    def nrm(i, shape, scale):
        return jax.random.normal(keys[i], shape, f32) * scale

    n_dense = (DEPTH + 1) // 2
    n_moe = DEPTH // 2
    gate_offset = jnp.zeros((6, D_MODEL), f32).at[jnp.array([2, 5])].set(1.0).reshape(6 * D_MODEL)
    return {
        'x_prompt': nrm(0, (BATCH, SEQ, D_MODEL), 1.0),
        'x_sample': nrm(1, (DEC_BATCH, DEC_SEQ, D_MODEL), 1.0),
        'state_conv_a': nrm(2, (DEPTH, DEC_BATCH, CONV_A - 1, D_A), 0.5),
        'state_conv_c': nrm(3, (DEPTH, DEC_BATCH, CONV_C - 1, D_C), 1.0),
        'c_prompt': nrm(4, (BATCH, D_MODEL), 1.0),
        'c_sample': nrm(5, (DEC_BATCH, D_MODEL), 1.0),
        'w_mod': nrm(6, (DEPTH, D_MODEL, 6 * D_MODEL), 0.1 * D_MODEL ** -0.5),
        'b_mod': nrm(7, (DEPTH, 6 * D_MODEL), 0.02) + gate_offset,
        'g_norm1': 1.0 + nrm(8, (DEPTH, D_MODEL), 0.02),
        'g_norm2': 1.0 + nrm(9, (DEPTH, D_MODEL), 0.02),
        'w_in': nrm(10, (DEPTH, D_MODEL, D_IN), D_MODEL ** -0.5),
        'w_out': nrm(11, (DEPTH, D_MIX, D_MODEL), D_MIX ** -0.5),
        'conv_a_w': nrm(12, (DEPTH, CONV_A, D_A), CONV_A ** -0.5),
        'conv_a_b': nrm(13, (DEPTH, D_A), 0.02),
        'ln_a_g': 1.0 + nrm(14, (DEPTH, D_A), 0.02),
        'ln_a_b': nrm(15, (DEPTH, D_A), 0.02),
        'ln_v_g': 1.0 + nrm(16, (DEPTH, D_B), 0.02),
        'ln_v_b': nrm(17, (DEPTH, D_B), 0.02),
        'w_spatial': nrm(18, (DEPTH, N_HEADS_B, CHUNK, CHUNK), CHUNK ** -0.5),
        'b_spatial': nrm(19, (DEPTH, N_HEADS_B, CHUNK), 0.02),
        'conv_c_w': nrm(20, (DEPTH, CONV_C, D_C), CONV_C ** -0.5),
        'w_gate_dense': nrm(21, (n_dense, D_MODEL, D_FF), D_MODEL ** -0.5),
        'w_up_dense': nrm(22, (n_dense, D_MODEL, D_FF), D_MODEL ** -0.5),
        'w_down_dense': nrm(23, (n_dense, D_FF, D_MODEL), D_FF ** -0.5),
        'w_router': nrm(24, (n_moe, D_MODEL, N_EXPERTS), D_MODEL ** -0.5),
        'w_gate_exp': nrm(25, (n_moe, N_EXPERTS, D_MODEL, D_FF_EXPERT), D_MODEL ** -0.5),
        'w_up_exp': nrm(26, (n_moe, N_EXPERTS, D_MODEL, D_FF_EXPERT), D_MODEL ** -0.5),
        'w_down_exp': nrm(27, (n_moe, N_EXPERTS, D_FF_EXPERT, D_MODEL), D_FF_EXPERT ** -0.5),
        'g_final': 1.0 + nrm(28, (D_MODEL,), 0.02),
    }


def reference(x_prompt, x_sample, state_conv_a, state_conv_c, c_prompt, c_sample,
              w_mod, b_mod, g_norm1, g_norm2, w_in, w_out, conv_a_w, conv_a_b,
              ln_a_g, ln_a_b, ln_v_g, ln_v_b, w_spatial, b_spatial, conv_c_w,
              w_gate_dense, w_up_dense, w_down_dense, w_router, w_gate_exp,
              w_up_exp, w_down_exp, g_final):
    weights = (w_mod, b_mod, g_norm1, g_norm2, w_in, w_out, conv_a_w, conv_a_b,
               ln_a_g, ln_a_b, ln_v_g, ln_v_b, w_spatial, b_spatial, conv_c_w,
               w_gate_dense, w_up_dense, w_down_dense, w_router, w_gate_exp,
               w_up_exp, w_down_exp, g_final)
    n_p = x_prompt.shape[0]
    zero_a = jnp.zeros((DEPTH, n_p, CONV_A - 1, D_A), x_prompt.dtype)
    zero_c = jnp.zeros((DEPTH, n_p, CONV_C - 1, D_C), x_prompt.dtype)
    y_prompt, conv_a_prompt, conv_c_prompt, _ = _trunk(x_prompt, c_prompt, zero_a, zero_c, *weights)
    y_sample, conv_a_sample, conv_c_sample, chunk_v_sample = _trunk(
        x_sample, c_sample, state_conv_a, state_conv_c, *weights)
    return (y_prompt, y_sample, conv_a_prompt, conv_a_sample, conv_c_prompt, conv_c_sample, chunk_v_sample)
```

```python
import functools

import jax
import jax.numpy as jnp
from jax import lax
from jax.experimental import pallas as pl
from jax.experimental.pallas import tpu as pltpu

f32 = jnp.float32
bf16 = jnp.bfloat16
i32 = jnp.int32

D_MODEL = 4096
HEAD_DIM = 128
N_HEADS_B = 10
D_A = 1536
D_B = 1280
D_C = 1280
CONV_A = 31
CONV_C = 3
CHUNK = 128
D_IN = 2 * D_A + 2 * D_B + 3 * D_C
OFF_AV, OFF_AG, OFF_U, OFF_V, OFF_BG, OFF_CG, OFF_HC = 0, 1536, 3072, 4352, 5632, 6912, 8192
N_EXPERTS = 8
TOP_K = 2
RMS_EPS = 1e-6
LN_EPS = 1e-5

V7X_VMEM_BYTES = 64 * 1024 * 1024
LANES = 128
SUBLANES = 8
MIB = 1024 * 1024

ROW_TILE = 128
MM_TM = 1088
MM_TN = 512
FFN_TM = 1024
FFN_SUB = 256
FFN_TF = 256
MIX_TL = 256
SAMPLE_BT = 16
GATHER_ROWS = 128
HIST_A = 32
HIST_C = 8


def _params(vmem_mib, sem=None):
    return pltpu.CompilerParams(dimension_semantics=sem, vmem_limit_bytes=vmem_mib * MIB)


def _sigmoid(x):
    return jax.nn.sigmoid(x)


def _silu(x):
    return x * jax.nn.sigmoid(x)


def _mod_kernel(c_ref, w_ref, b_ref, o_ref, sc_ref):
    @pl.when(pl.program_id(1) == 0)
    def _():
        sc_ref[...] = _silu(c_ref[...]).astype(bf16)

    o_ref[...] = jnp.dot(sc_ref[...], w_ref[...].astype(bf16), preferred_element_type=f32) + b_ref[...]


def _modulation(c_all, w_mod, b_mod):
    depth, d, n = w_mod.shape
    m = c_all.shape[0]
    tn = 512
    return pl.pallas_call(
        _mod_kernel,
        grid=(depth, n // tn),
        in_specs=[pl.BlockSpec((m, d), lambda l, j: (0, 0)),
                  pl.BlockSpec((None, d, tn), lambda l, j: (l, 0, j)),
                  pl.BlockSpec((None, 1, tn), lambda l, j: (l, 0, j))],
        out_specs=pl.BlockSpec((None, m, tn), lambda l, j: (l, 0, j)),
        out_shape=jax.ShapeDtypeStruct((depth, m, n), f32),
        scratch_shapes=[pltpu.VMEM((m, d), bf16)],
        compiler_params=_params(52, ("arbitrary", "arbitrary")),
        name="modulation",
    )(c_all, w_mod, b_mod.reshape(depth, 1, n))


def _rownorm_kernel(*refs, n_prompt_tiles, tiles_per_seq, has_resid, has_mod, emit_x):
    it = iter(refs)
    x_ref = next(it)
    if has_resid:
        f_ref, gp_ref, gs_ref = next(it), next(it), next(it)
    g_ref = next(it)
    if has_mod:
        scp_ref, scs_ref, shp_ref, shs_ref = next(it), next(it), next(it), next(it)
    xo_ref = next(it) if emit_x else None
    h_ref = next(it)

    i = pl.program_id(0)
    is_prompt = i < n_prompt_tiles
    b = jnp.minimum(i // tiles_per_seq, SUBLANES - 1)

    def pick(p_ref, s_ref):
        return jnp.where(is_prompt, p_ref[pl.ds(b, 1), :], s_ref[...])

    x = x_ref[...]
    if has_resid:
        x = x + pick(gp_ref, gs_ref) * f_ref[...]
    if emit_x:
        xo_ref[...] = x
    y = x * lax.rsqrt(jnp.mean(x * x, axis=-1, keepdims=True) + RMS_EPS)
    y = y * g_ref[...]
    if has_mod:
        y = y * (1.0 + pick(scp_ref, scs_ref)) + pick(shp_ref, shs_ref)
    h_ref[...] = y.astype(h_ref.dtype)


def _rownorm(x, g, mod_l, n_prompt_rows, seq_len, *, f=None, gate_mod=None, gate_col=None,
             sc_col=None, sh_col=None, emit_x=True, h_dtype=bf16):
    m, d = x.shape
    tr = ROW_TILE
    n_prompt_tiles = n_prompt_rows // tr
    n_sample_rows = m - n_prompt_rows
    n_sample_tiles = n_sample_rows // tr
    prompt_blk = n_sample_rows // SUBLANES
    has_resid = f is not None
    has_mod = sc_col is not None

    def p_spec(col):
        return pl.BlockSpec((SUBLANES, d), lambda i: (prompt_blk, col))

    def s_spec(col):
        return pl.BlockSpec(
            (tr, d), lambda i: (jnp.clip(i - n_prompt_tiles, 0, n_sample_tiles - 1), col))

    row = pl.BlockSpec((tr, d), lambda i: (i, 0))
    ins, specs = [x], [row]
    if has_resid:
        ins += [f, gate_mod, gate_mod]
        specs += [row, p_spec(gate_col), s_spec(gate_col)]
    ins.append(g.reshape(1, d))
    specs.append(pl.BlockSpec((1, d), lambda i: (0, 0)))
    if has_mod:
        ins += [mod_l, mod_l, mod_l, mod_l]
        specs += [p_spec(sc_col), s_spec(sc_col), p_spec(sh_col), s_spec(sh_col)]
    out_shape, out_specs = [], []
    if emit_x:
        out_shape.append(jax.ShapeDtypeStruct((m, d), f32))
        out_specs.append(row)
    out_shape.append(jax.ShapeDtypeStruct((m, d), h_dtype))
    out_specs.append(row)
    res = pl.pallas_call(
        functools.partial(_rownorm_kernel, n_prompt_tiles=n_prompt_tiles,
                          tiles_per_seq=seq_len // tr, has_resid=has_resid, has_mod=has_mod,
                          emit_x=emit_x),
        grid=(m // tr,),
        in_specs=specs,
        out_specs=out_specs,
        out_shape=out_shape,
        compiler_params=_params(48, ("arbitrary",)),
        name="rownorm",
    )(*ins)
    return res if emit_x else (None, res[0])


def _mm_kernel(a_ref, w_ref, o_ref):
    o_ref[...] = jnp.dot(a_ref[...], w_ref[...].astype(bf16), preferred_element_type=f32)


def _matmul(a, w):
    m, k = a.shape
    n = w.shape[1]
    tm, tn = MM_TM, MM_TN
    assert m % tm == 0
    return pl.pallas_call(
        _mm_kernel,
        grid=(m // tm, pl.cdiv(n, tn)),
        in_specs=[pl.BlockSpec((tm, k), lambda i, j: (i, 0)),
                  pl.BlockSpec((k, tn), lambda i, j: (0, j))],
        out_specs=pl.BlockSpec((tm, tn), lambda i, j: (i, j)),
        out_shape=jax.ShapeDtypeStruct((m, n), f32),
        compiler_params=_params(56, ("arbitrary", "arbitrary")),
        name="matmul",
    )(a, w)


def _layernorm(x, g, b):
    mu = jnp.mean(x, axis=-1, keepdims=True)
    xc = x - mu
    var = jnp.mean(xc * xc, axis=-1, keepdims=True)
    return xc * lax.rsqrt(var + LN_EPS) * g + b


def _mix_prompt_kernel(p_ref, caw_ref, cab_ref, lag_ref, lab_ref, lvg_ref, lvb_ref, ws_ref,
                       bs_ref, ccw_ref, o_ref, sa_ref, sc_ref, exta, extc):
    j = pl.program_id(1)
    tl = MIX_TL

    @pl.when(j == 0)
    def _():
        exta[pl.ds(0, HIST_A), :] = jnp.zeros((HIST_A, D_A), f32)
        extc[pl.ds(0, HIST_C), :] = jnp.zeros((HIST_C, D_C), f32)

    @pl.when(j > 0)
    def _():
        exta[pl.ds(0, HIST_A), :] = exta[pl.ds(tl, HIST_A), :]
        extc[pl.ds(0, HIST_C), :] = extc[pl.ds(tl, HIST_C), :]

    a_glu = p_ref[:, OFF_AV:OFF_AV + D_A] * _sigmoid(p_ref[:, OFF_AG:OFF_AG + D_A])
    exta[pl.ds(HIST_A, tl), :] = a_glu
    acc = jnp.zeros((tl, D_A), f32)
    for k in range(CONV_A):
        acc = acc + caw_ref[pl.ds(k, 1), :] * exta[pl.ds(HIST_A - (CONV_A - 1) + k, tl), :]
    a = _silu(_layernorm(acc + cab_ref[...], lag_ref[...], lab_ref[...]))
    o_ref[:, 0:D_A] = a.astype(o_ref.dtype)

    v = _layernorm(p_ref[:, OFF_V:OFF_V + D_B], lvg_ref[...], lvb_ref[...]).astype(bf16)
    row = lax.broadcasted_iota(i32, (CHUNK, CHUNK), 0)
    col = lax.broadcasted_iota(i32, (CHUNK, CHUNK), 1)
    for h in range(N_HEADS_B):
        w = jnp.where(col <= row, ws_ref[h], 0.0).astype(bf16)
        lo = h * HEAD_DIM
        for c in range(tl // CHUNK):
            r0 = c * CHUNK
            mixed = jnp.dot(w, v[r0:r0 + CHUNK, lo:lo + HEAD_DIM], preferred_element_type=f32)
            mixed = mixed + bs_ref[h]
            u = p_ref[r0:r0 + CHUNK, OFF_U + lo:OFF_U + lo + HEAD_DIM]
            o_ref[r0:r0 + CHUNK, D_A + lo:D_A + lo + HEAD_DIM] = (u * mixed).astype(o_ref.dtype)

    s = p_ref[:, OFF_CG:OFF_CG + D_C] * p_ref[:, OFF_HC:OFF_HC + D_C]
    extc[pl.ds(HIST_C, tl), :] = s
    accc = jnp.zeros((tl, D_C), f32)
    for k in range(CONV_C):
        accc = accc + ccw_ref[pl.ds(k, 1), :] * extc[pl.ds(HIST_C - (CONV_C - 1) + k, tl), :]
    o_ref[:, D_A + D_B:] = (p_ref[:, OFF_BG:OFF_BG + D_C] * accc).astype(o_ref.dtype)

    @pl.when(j == pl.num_programs(1) - 1)
    def _():
        sa_ref[...] = exta[pl.ds(HIST_A + tl - (CONV_A - 1), CONV_A - 1), :]
        sc_ref[...] = extc[pl.ds(HIST_C + tl - (CONV_C - 1), CONV_C - 1), :]


def _mix_prompt(proj, n_seq, seq_len, caw, cab, lag, lab, lvg, lvb, ws, bs, ccw):
    tl = MIX_TL
    nj = seq_len // tl
    full = lambda shape: pl.BlockSpec(shape, lambda b, j: (0,) * len(shape))
    return pl.pallas_call(
        _mix_prompt_kernel,
        grid=(n_seq, nj),
        in_specs=[pl.BlockSpec((tl, D_IN), lambda b, j: (b * nj + j, 0)),
                  full((CONV_A, D_A)), full((1, D_A)), full((1, D_A)), full((1, D_A)),
                  full((1, D_B)), full((1, D_B)), full((N_HEADS_B, CHUNK, CHUNK)),
                  full((N_HEADS_B, CHUNK, 1)), full((CONV_C, D_C))],
        out_specs=[pl.BlockSpec((tl, D_MODEL), lambda b, j: (b * nj + j, 0)),
                   pl.BlockSpec((None, CONV_A - 1, D_A), lambda b, j: (b, 0, 0)),
                   pl.BlockSpec((None, CONV_C - 1, D_C), lambda b, j: (b, 0, 0))],
        out_shape=[jax.ShapeDtypeStruct((n_seq * seq_len, D_MODEL), bf16),
                   jax.ShapeDtypeStruct((n_seq, CONV_A - 1, D_A), f32),
                   jax.ShapeDtypeStruct((n_seq, CONV_C - 1, D_C), f32)],
        scratch_shapes=[pltpu.VMEM((HIST_A + tl, D_A), f32), pltpu.VMEM((HIST_C + tl, D_C), f32)],
        compiler_params=_params(48, ("arbitrary", "arbitrary")),
        name="mix_prompt",
    )(proj, caw, cab.reshape(1, D_A), lag.reshape(1, D_A), lab.reshape(1, D_A),
      lvg.reshape(1, D_B), lvb.reshape(1, D_B), ws, bs.reshape(N_HEADS_B, CHUNK, 1), ccw)


def _mix_sample_kernel(p_ref, sta_ref, stc_ref, caw_ref, cab_ref, lag_ref, lab_ref, lvg_ref,
                       lvb_ref, wd_ref, bd_ref, ccw_ref, o_ref, sa_ref, sc_ref, cv_ref,
                       exta, extv, extc, *, n_new):
    n = n_new
    extv[pl.ds(0, SUBLANES), :] = jnp.zeros((SUBLANES, D_B), f32)

    def body(b, carry):
        a_glu = p_ref[b, :, OFF_AV:OFF_AV + D_A] * _sigmoid(p_ref[b, :, OFF_AG:OFF_AG + D_A])
        exta[pl.ds(0, CONV_A - 1), :] = sta_ref[b]
        exta[pl.ds(CONV_A - 1, n), :] = a_glu
        acc = jnp.zeros((n, D_A), f32)
        for k in range(CONV_A):
            acc = acc + caw_ref[pl.ds(k, 1), :] * exta[pl.ds(k, n), :]
        a = _silu(_layernorm(acc + cab_ref[...], lag_ref[...], lab_ref[...]))
        o_ref[b, :, 0:D_A] = a
        sa_ref[b] = exta[pl.ds(n, CONV_A - 1), :]

        v = _layernorm(p_ref[b, :, OFF_V:OFF_V + D_B], lvg_ref[...], lvb_ref[...])
        cv_ref[b] = v
        extv[pl.ds(SUBLANES, n), :] = v
        mixed = bd_ref[...]
        for d in range(n):
            mixed = mixed + wd_ref[d] * extv[pl.ds(SUBLANES - d, n), :]
        o_ref[b, :, D_A:D_A + D_B] = p_ref[b, :, OFF_U:OFF_U + D_B] * mixed

        s = p_ref[b, :, OFF_CG:OFF_CG + D_C] * p_ref[b, :, OFF_HC:OFF_HC + D_C]
        extc[pl.ds(0, CONV_C - 1), :] = stc_ref[b]
        extc[pl.ds(CONV_C - 1, n), :] = s
        accc = jnp.zeros((n, D_C), f32)
        for k in range(CONV_C):
            accc = accc + ccw_ref[pl.ds(k, 1), :] * extc[pl.ds(k, n), :]
        o_ref[b, :, D_A + D_B:] = p_ref[b, :, OFF_BG:OFF_BG + D_C] * accc
        sc_ref[b] = extc[pl.ds(n, CONV_C - 1), :]
        return carry

    lax.fori_loop(0, SAMPLE_BT, body, 0)


def _mix_sample(proj3, st_a, st_c, caw, cab, lag, lab, lvg, lvb, w_diag, b_rows, ccw):
    nb, n_new, _ = proj3.shape
    bt = SAMPLE_BT
    full = lambda shape: pl.BlockSpec(shape, lambda i: (0,) * len(shape))
    blk = lambda r, c: pl.BlockSpec((bt, r, c), lambda i: (i, 0, 0))
    return pl.pallas_call(
        functools.partial(_mix_sample_kernel, n_new=n_new),
        grid=(nb // bt,),
        in_specs=[blk(n_new, D_IN), blk(CONV_A - 1, D_A), blk(CONV_C - 1, D_C),
                  full((CONV_A, D_A)), full((1, D_A)), full((1, D_A)), full((1, D_A)),
                  full((1, D_B)), full((1, D_B)), full((n_new, n_new, D_B)), full((n_new, D_B)),
                  full((CONV_C, D_C))],
        out_specs=[blk(n_new, D_MODEL), blk(CONV_A - 1, D_A), blk(CONV_C - 1, D_C),
                   blk(n_new, D_B)],
        out_shape=[jax.ShapeDtypeStruct((nb, n_new, D_MODEL), f32),
                   jax.ShapeDtypeStruct((nb, CONV_A - 1, D_A), f32),
                   jax.ShapeDtypeStruct((nb, CONV_C - 1, D_C), f32),
                   jax.ShapeDtypeStruct((nb, n_new, D_B), f32)],
        scratch_shapes=[pltpu.VMEM((CONV_A - 1 + SUBLANES, D_A), f32),
                        pltpu.VMEM((2 * SUBLANES, D_B), f32),
                        pltpu.VMEM((2 * SUBLANES, D_C), f32)],
        compiler_params=_params(48, ("arbitrary",)),
        name="mix_sample",
    )(proj3, st_a, st_c, caw, cab.reshape(1, D_A), lag.reshape(1, D_A), lab.reshape(1, D_A),
      lvg.reshape(1, D_B), lvb.reshape(1, D_B), w_diag, b_rows, ccw)


def _sample_spatial_tables(ws, bs, n_new):
    w = ws[:, :n_new, :n_new]
    p = jnp.arange(n_new)
    diag = []
    for d in range(n_new):
        q = jnp.clip(p - d, 0, n_new - 1)
        wd = jnp.where((p >= d)[None, :], w[:, p, q], 0.0)
        diag.append(jnp.repeat(wd.T, HEAD_DIM, axis=1))
    b_rows = jnp.repeat(bs[:, :n_new].T, HEAD_DIM, axis=1)
    return jnp.stack(diag), b_rows


def _ffn_kernel(te_ref, ns_ref, tb_ref, x_ref, wg_ref, wu_ref, wd_ref, *rest, scale_rows):
    if scale_rows:
        gate_ref, o_ref, wgb, wub, wdb = rest
    else:
        o_ref, wgb, wub, wdb = rest
    i = pl.program_id(0)
    j = pl.program_id(1)
    nsub = ns_ref[i]

    @pl.when(nsub > 0)
    def _():
        @pl.when(j == 0)
        def _():
            o_ref[...] = jnp.zeros_like(o_ref)

        wgb[...] = wg_ref[...].astype(bf16)
        wub[...] = wu_ref[...].astype(bf16)
        wdb[...] = wd_ref[...].astype(bf16)

        def body(s, carry):
            r = pl.multiple_of(s * FFN_SUB, FFN_SUB)
            xs = x_ref[pl.ds(r, FFN_SUB), :]
            g = jnp.dot(xs, wgb[...], preferred_element_type=f32)
            u = jnp.dot(xs, wub[...], preferred_element_type=f32)
            h = (_silu(g) * u).astype(bf16)
            o_ref[pl.ds(r, FFN_SUB), :] += jnp.dot(h, wdb[...], preferred_element_type=f32)
            return carry

        lax.fori_loop(0, nsub, body, 0)

        if scale_rows:
            @pl.when(j == pl.num_programs(1) - 1)
            def _():
                o_ref[...] = o_ref[...] * gate_ref[...]


def _ffn(x, wg, wu, wd, tile_expert, tile_nsub, tile_blk, row_gate=None):
    rows, d = x.shape
    n_exp, _, d_ff = wg.shape
    tm, tf = FFN_TM, FFN_TF
    assert d_ff % tf == 0
    nj = d_ff // tf
    n_tiles = tile_expert.shape[0]
    scale_rows = row_gate is not None

    def jj(i, j, ns):
        return jnp.where(ns[i] > 0, j, nj - 1)

    single = pl.Buffered(1)
    in_specs = [
        pl.BlockSpec((tm, d), lambda i, j, te, ns, tb: (tb[i], 0), pipeline_mode=single),
        pl.BlockSpec((None, d, tf), lambda i, j, te, ns, tb: (te[i], 0, jj(i, j, ns))),
        pl.BlockSpec((None, d, tf), lambda i, j, te, ns, tb: (te[i], 0, jj(i, j, ns))),
        pl.BlockSpec((None, tf, d), lambda i, j, te, ns, tb: (te[i], jj(i, j, ns), 0)),
    ]
    ins = [x, wg, wu, wd]
    if scale_rows:
        in_specs.append(pl.BlockSpec((tm, 1), lambda i, j, te, ns, tb: (tb[i], 0)))
        ins.append(row_gate)
    return pl.pallas_call(
        functools.partial(_ffn_kernel, scale_rows=scale_rows),
        grid_spec=pltpu.PrefetchScalarGridSpec(
            num_scalar_prefetch=3,
            grid=(n_tiles, nj),
            in_specs=in_specs,
            out_specs=pl.BlockSpec((tm, d), lambda i, j, te, ns, tb: (tb[i], 0),
                                   pipeline_mode=single),
            scratch_shapes=[pltpu.VMEM((d, tf), bf16), pltpu.VMEM((d, tf), bf16),
                            pltpu.VMEM((tf, d), bf16)],
        ),
        out_shape=jax.ShapeDtypeStruct((rows, d), f32),
        compiler_params=_params(60, ("arbitrary", "arbitrary")),
        name="ffn",
    )(tile_expert, tile_nsub, tile_blk, *ins)


def _router_kernel(h_ref, w_ref, o_ref):
    logits = jnp.dot(h_ref[...], w_ref[...], preferred_element_type=f32,
                     precision=lax.Precision.HIGHEST)
    lane = lax.broadcasted_iota(i32, logits.shape, 1)
    neg = jnp.float32(-jnp.inf)
    logits = jnp.where(lane < N_EXPERTS, logits, neg)
    lane_f = lane.astype(f32)
    v1 = jnp.max(logits, axis=-1, keepdims=True)
    i1 = jnp.min(jnp.where(logits == v1, lane_f, float(LANES)), axis=-1, keepdims=True)
    rest = jnp.where(lane_f == i1, neg, logits)
    v2 = jnp.max(rest, axis=-1, keepdims=True)
    i2 = jnp.min(jnp.where(rest == v2, lane_f, float(LANES)), axis=-1, keepdims=True)
    e2 = jnp.exp(v2 - v1)
    den = 1.0 + e2
    g1 = 1.0 / den
    g2 = e2 / den
    out = jnp.where(lane == 0, i1,
                    jnp.where(lane == 1, i2,
                              jnp.where(lane == 2, g1, jnp.where(lane == 3, g2, 0.0))))
    o_ref[...] = out


def _router(h, w_router_padded):
    m, d = h.shape
    tr = 512
    return pl.pallas_call(
        _router_kernel,
        grid=(m // tr,),
        in_specs=[pl.BlockSpec((tr, d), lambda i: (i, 0)),
                  pl.BlockSpec((d, LANES), lambda i: (0, 0))],
        out_specs=pl.BlockSpec((tr, LANES), lambda i: (i, 0)),
        out_shape=jax.ShapeDtypeStruct((m, LANES), f32),
        compiler_params=_params(48, ("arbitrary",)),
        name="router",
    )(h, w_router_padded)


def _gather_rows_kernel(idx_ref, src_hbm, o_ref, buf, sem, *, n_src, rows):
    i = pl.program_id(0)
    n_steps = pl.num_programs(0)

    n_out = n_steps * rows

    def copies(step, slot, r):
        return [pltpu.make_async_copy(src_hbm.at[pl.ds(idx_ref[k * n_out + step * rows + r], 1), :],
                                      buf.at[slot, k, pl.ds(r, 1), :], sem.at[slot])
                for k in range(n_src)]

    def start(step, slot):
        def issue(r, c):
            for cp in copies(step, slot, r):
                cp.start()
            return c
        lax.fori_loop(0, rows, issue, 0)

    def wait(step, slot):
        def w(r, c):
            for cp in copies(step, slot, r):
                cp.wait()
            return c
        lax.fori_loop(0, rows, w, 0)

    slot = i % 2

    @pl.when(i == 0)
    def _():
        start(0, 0)

    @pl.when(i + 1 < n_steps)
    def _():
        start(i + 1, 1 - slot)

    wait(i, slot)
    acc = buf[slot, 0]
    for k in range(1, n_src):
        acc = acc + buf[slot, k]
    o_ref[...] = acc.astype(o_ref.dtype)


def _gather_rows(idx, src, out_dtype):
    n_src, n_out = idx.shape
    d = src.shape[1]
    rows = GATHER_ROWS
    assert n_out % rows == 0
    return pl.pallas_call(
        functools.partial(_gather_rows_kernel, n_src=n_src, rows=rows),
        grid_spec=pltpu.PrefetchScalarGridSpec(
            num_scalar_prefetch=1,
            grid=(n_out // rows,),
            in_specs=[pl.BlockSpec(memory_space=pl.ANY)],
            out_specs=pl.BlockSpec((rows, d), lambda i, idx: (i, 0)),
            scratch_shapes=[pltpu.VMEM((2, n_src, rows, d), f32),
                            pltpu.SemaphoreType.DMA((2,))],
        ),
        out_shape=jax.ShapeDtypeStruct((n_out, d), out_dtype),
        compiler_params=_params(32, ("arbitrary",)),
        name="gather_rows",
    )(idx.reshape(n_src * n_out), src)


def _routing_tables(meta, n_tiles):
    t = meta.shape[0]
    top_idx = meta[:, 0:TOP_K].astype(i32)
    gates = meta[:, TOP_K:2 * TOP_K]
    a = t * TOP_K
    exp_flat = top_idx.reshape(a)
    gate_flat = gates.reshape(a)
    tok_flat = jnp.arange(a, dtype=i32) // TOP_K
    onehot = (exp_flat[:, None] == jnp.arange(N_EXPERTS, dtype=i32)[None, :]).astype(i32)
    csum = jnp.cumsum(onehot, axis=0)
    counts = csum[-1]
    rank = jnp.take_along_axis(csum, exp_flat[:, None], axis=1)[:, 0] - 1
    padded = ((counts + FFN_TM - 1) // FFN_TM) * FFN_TM
    pad_ends = jnp.cumsum(padded)
    pad_starts = pad_ends - padded
    dest = pad_starts[exp_flat] + rank
    cap = n_tiles * FFN_TM
    slot_tok = jnp.zeros((cap,), i32).at[dest].set(tok_flat)
    slot_gate = jnp.zeros((cap,), f32).at[dest].set(gate_flat)
    tile_start = jnp.arange(n_tiles, dtype=i32) * FFN_TM
    n_used = pad_ends[-1] // FFN_TM
    tile_valid = jnp.arange(n_tiles, dtype=i32) < n_used
    tile_blk = jnp.minimum(jnp.arange(n_tiles, dtype=i32), n_used - 1)
    tile_e = jnp.minimum(
        jnp.searchsorted(pad_ends, tile_blk * FFN_TM, side='right'), N_EXPERTS - 1).astype(i32)
    left = counts[tile_e] - (tile_start - pad_starts[tile_e])
    tile_nsub = jnp.where(tile_valid,
                          jnp.clip((left + FFN_SUB - 1) // FFN_SUB, 0, FFN_TM // FFN_SUB), 0)
    return (slot_tok, slot_gate, dest.reshape(t, TOP_K), tile_e, tile_nsub.astype(i32),
            tile_blk.astype(i32))


def _moe(h, w_router, w_gate, w_up, w_down):
    t, d = h.shape
    n_tiles = (t * TOP_K) // FFN_TM + N_EXPERTS
    w_pad = jnp.zeros((d, LANES), f32).at[:, :N_EXPERTS].set(w_router)
    meta = _router(h, w_pad)
    slot_tok, slot_gate, tok_slots, tile_e, tile_nsub, tile_blk = _routing_tables(meta, n_tiles)
    xs = _gather_rows(slot_tok.reshape(1, -1), h, bf16)
    ys = _ffn(xs, w_gate, w_up, w_down, tile_e, tile_nsub, tile_blk,
              row_gate=slot_gate.reshape(-1, 1))
    return _gather_rows(tok_slots.T, ys, f32)


def kernel(x_prompt, x_sample, state_conv_a, state_conv_c, c_prompt, c_sample, w_mod, b_mod, g_norm1, g_norm2, w_in, w_out, conv_a_w, conv_a_b, ln_a_g, ln_a_b, ln_v_g, ln_v_b, w_spatial, b_spatial, conv_c_w, w_gate_dense, w_up_dense, w_down_dense, w_router, w_gate_exp, w_up_exp, w_down_exp, g_final):
    n_p, seq, d = x_prompt.shape
    n_s, n_new, _ = x_sample.shape
    depth = w_mod.shape[0]
    rows_p = n_p * seq
    rows_s = n_s * n_new
    rows = rows_p + rows_s

    x = jnp.concatenate([x_prompt.reshape(rows_p, d), x_sample.reshape(rows_s, d)], axis=0)
    c_all = jnp.concatenate([jnp.repeat(c_sample, n_new, axis=0), c_prompt,
                             jnp.zeros((SUBLANES - n_p, d), f32)], axis=0)
    mod = _modulation(c_all, w_mod, b_mod)

    dense_tiles = pl.cdiv(rows, FFN_TM)
    dense_e = jnp.zeros((dense_tiles,), i32)
    dense_nsub = jnp.minimum((rows - jnp.arange(dense_tiles, dtype=i32) * FFN_TM) // FFN_SUB,
                             FFN_TM // FFN_SUB).astype(i32)
    dense_blk = jnp.arange(dense_tiles, dtype=i32)

    norm = functools.partial(_rownorm, n_prompt_rows=rows_p, seq_len=seq)
    new_a_p, new_a_s, new_c_p, new_c_s, new_v = [], [], [], [], []
    f = None
    for l in range(depth):
        mod_l = mod[l]
        if l == 0:
            _, h = norm(x, g_norm1[l], mod_l, sc_col=1, sh_col=0, emit_x=False)
        else:
            x, h = norm(x, g_norm1[l], mod_l, f=f, gate_mod=mod[l - 1], gate_col=5, sc_col=1,
                        sh_col=0)
        proj = _matmul(h, w_in[l])
        mix_p, a_p, c_p = _mix_prompt(proj, n_p, seq, conv_a_w[l], conv_a_b[l], ln_a_g[l],
                                      ln_a_b[l], ln_v_g[l], ln_v_b[l], w_spatial[l],
                                      b_spatial[l], conv_c_w[l])
        w_diag, b_rows = _sample_spatial_tables(w_spatial[l], b_spatial[l], n_new)
        mix_s, a_s, c_s, v_s = _mix_sample(proj[rows_p:].reshape(n_s, n_new, D_IN),
                                           state_conv_a[l], state_conv_c[l], conv_a_w[l],
                                           conv_a_b[l], ln_a_g[l], ln_a_b[l], ln_v_g[l],
                                           ln_v_b[l], w_diag, b_rows, conv_c_w[l])
        mixed = jnp.concatenate([mix_p, mix_s.reshape(rows_s, d).astype(bf16)], axis=0)
        mix = _matmul(mixed, w_out[l])
        is_moe = l % 2 == 1
        x, h2 = norm(x, g_norm2[l], mod_l, f=mix, gate_mod=mod_l, gate_col=2, sc_col=4,
                     sh_col=3, h_dtype=f32 if is_moe else bf16)
        if is_moe:
            f = _moe(h2, w_router[l // 2], w_gate_exp[l // 2], w_up_exp[l // 2],
                     w_down_exp[l // 2])
        else:
            f = _ffn(h2, w_gate_dense[l // 2][None], w_up_dense[l // 2][None],
                     w_down_dense[l // 2][None], dense_e, dense_nsub, dense_blk)
        new_a_p.append(a_p)
        new_a_s.append(a_s)
        new_c_p.append(c_p)
        new_c_s.append(c_s)
        new_v.append(v_s)
    _, y = _rownorm(x, g_final, None, rows_p, seq, f=f, gate_mod=mod[depth - 1], gate_col=5,
                    emit_x=False, h_dtype=f32)
    y_prompt = y[:rows_p].reshape(n_p, seq, d)
    y_sample = y[rows_p:].reshape(n_s, n_new, d)
    return (y_prompt, y_sample, jnp.stack(new_a_p), jnp.stack(new_a_s), jnp.stack(new_c_p),
            jnp.stack(new_c_s), jnp.stack(new_v))
```
